```python
import functools
import jax
import jax.numpy as jnp
from jax import lax
import numpy as np

D_MODEL = 2048
BATCH = 2
SEQ = 4096
DEPTH = 4

CTX_LEN = 256
GRID_W = 64
ROPE_BASE = 10000.0
NORM_EPS = 1e-6
NEG_INF = -1e30
QBLOCK = 128

MLA_HEADS = 4
MLA_NOPE = 128
MLA_ROPE = 64
MLA_V = 128
MLA_Q_LORA = 512
MLA_KV_LORA = 256
MLA_SCALE = (MLA_NOPE + MLA_ROPE) ** -0.5
POOL_WINDOWS = (2, 4, 8, 16)
POOL_GROUP = 128
POOL_WIDTH = POOL_GROUP * len(POOL_WINDOWS)
SWA_HEADS = 8
SWA_KV_HEADS = 2
SWA_HEAD_DIM = 64
SWA_WINDOW = 128
SWA_BLOCK = 128
NA_HEADS = 8
NA_HEAD_DIM = 64
NA_KH = 8
NA_KW = 16
NA_QC = 16
NA_KC = NA_QC + NA_KW
FFN_DIM = 5632

A_COLS = MLA_Q_LORA + MLA_KV_LORA + MLA_ROPE
B_COLS = POOL_WIDTH
C_COLS = (SWA_HEADS + 2 * SWA_KV_HEADS) * SWA_HEAD_DIM
D_COLS = 3 * NA_HEADS * NA_HEAD_DIM
IN_COLS = A_COLS + B_COLS + C_COLS + D_COLS
IN_SPLITS = (A_COLS, A_COLS + B_COLS, A_COLS + B_COLS + C_COLS)
MIX_OUT = MLA_HEADS * MLA_V + POOL_WIDTH + SWA_HEADS * SWA_HEAD_DIM + NA_HEADS * NA_HEAD_DIM

kernel_name = 'hybrid_pargroup_dit_block'

F32 = jnp.float32


def rms_norm(x, g):
    xf = x.astype(F32)
    y = xf * lax.rsqrt(jnp.mean(xf * xf, axis=-1, keepdims=True) + NORM_EPS)
    return (y * g.astype(F32)).astype(x.dtype)


def adaln(cond, w_mod, b_mod):
    return jnp.split(jax.nn.silu(cond) @ w_mod + b_mod, 6, axis=-1)


def modulate(h, shift, scale):
    return h * (1.0 + scale) + shift


def axial_angles(n, d_rot):
    t = jnp.arange(n)
    row = (t // GRID_W).astype(F32)
    col = (t % GRID_W).astype(F32)
    d_axis = d_rot // 2
    inv_freq = ROPE_BASE ** (-jnp.arange(0, d_axis, 2, dtype=F32) / d_axis)
    return (row[:, None] * inv_freq, col[:, None] * inv_freq)


def rope_segment(x, ang):
    cos = jnp.cos(ang)[:, None, :].astype(x.dtype)
    sin = jnp.sin(ang)[:, None, :].astype(x.dtype)
    x1, x2 = jnp.split(x, 2, axis=-1)
    return jnp.concatenate([x1 * cos - x2 * sin, x2 * cos + x1 * sin], axis=-1)


def axial_rope(x, ang):
    half = x.shape[-1] // 2
    return jnp.concatenate([rope_segment(x[..., :half], ang[0]), rope_segment(x[..., half:], ang[1])], axis=-1)


def map_query_blocks(fn, qs):
    b, n = qs[0].shape[:2]
    nb = n // QBLOCK
    blocks = tuple(jnp.moveaxis(q.reshape(b, nb, QBLOCK, *q.shape[2:]), 1, 0) for q in qs)
    out = lax.map(lambda qb: fn(*qb), blocks)
    return jnp.moveaxis(out, 0, 1).reshape(b, n, *out.shape[3:])


def mla_project(p, q_a_norm, w_qb, kv_a_norm, w_kvb, q_nope_norm, q_rope_norm, k_nope_norm, k_rope_norm, ang):
    b, n, _ = p.shape
    cq, ckv, kr = jnp.split(p, [MLA_Q_LORA, MLA_Q_LORA + MLA_KV_LORA], axis=-1)
    q = (rms_norm(cq, q_a_norm) @ w_qb).reshape(b, n, MLA_HEADS, MLA_NOPE + MLA_ROPE)
    kv = (rms_norm(ckv, kv_a_norm) @ w_kvb).reshape(b, n, MLA_HEADS, MLA_NOPE + MLA_V)
    q_nope = rms_norm(q[..., :MLA_NOPE], q_nope_norm)
    q_rope = rms_norm(q[..., MLA_NOPE:], q_rope_norm)
    k_nope = rms_norm(kv[..., :MLA_NOPE], k_nope_norm)
    v = kv[..., MLA_NOPE:]
    k_rope = rms_norm(kr, k_rope_norm)[:, :, None, :]
    if ang is not None:
        q_rope = axial_rope(q_rope, ang)
        k_rope = axial_rope(k_rope, ang)
    return q_nope, q_rope, k_nope, k_rope[:, :, 0, :], v


def mla_attend(q_nope, q_rope, k_nope, k_rope, v):
    s = (jnp.einsum('bqhd,bkhd->bhqk', q_nope, k_nope, preferred_element_type=F32)
         + jnp.einsum('bqhr,bkr->bhqk', q_rope, k_rope, preferred_element_type=F32))
    p = jax.nn.softmax(s * MLA_SCALE, axis=-1).astype(v.dtype)
    return jnp.einsum('bhqk,bkhd->bqhd', p, v)


def pool_mixer(u, w_pool, scale):
    b, n, _ = u.shape
    uf = u.astype(F32)
    csum = jnp.pad(jnp.cumsum(uf, axis=1), ((0, 0), (1, 0), (0, 0)))
    t = jnp.arange(n)
    diffs = []
    for g, w in enumerate(POOL_WINDOWS):
        sl = slice(g * POOL_GROUP, (g + 1) * POOL_GROUP)
        lo = jnp.clip(t - w // 2, 0, n)
        hi = jnp.clip(t + w // 2, 0, n)
        cs = csum[..., sl]
        mean = (cs[:, hi] - cs[:, lo]) / (hi - lo).astype(F32)[None, :, None]
        diffs.append(mean - uf[..., sl])
    d = jnp.stack(diffs, axis=2).astype(u.dtype)
    y = jnp.einsum('bngc,gcd->bngd', d, w_pool).reshape(b, n, POOL_WIDTH)
    return y * scale


def swa_project(p, q_norm, k_norm, ang):
    b, n, _ = p.shape
    q, k, v = jnp.split(p, [SWA_HEADS * SWA_HEAD_DIM, (SWA_HEADS + SWA_KV_HEADS) * SWA_HEAD_DIM], axis=-1)
    q = rms_norm(q.reshape(b, n, SWA_HEADS, SWA_HEAD_DIM), q_norm)
    k = rms_norm(k.reshape(b, n, SWA_KV_HEADS, SWA_HEAD_DIM), k_norm)
    v = v.reshape(b, n, SWA_KV_HEADS, SWA_HEAD_DIM)
    if ang is not None:
        q = axial_rope(q, ang)
        k = axial_rope(k, ang)
    return q, k, v


def swa_latent(q, k, v, k_ctx, v_ctx, sink):
    b, n, hq, d = q.shape
    hkv = k.shape[2]
    grp = hq // hkv
    blk = SWA_BLOCK
    nb = n // blk
    scale = d ** -0.5
    qg = q.reshape(b, nb, blk, hkv, grp, d)

    def band(t):
        tp = jnp.pad(t, ((0, 0), (blk, blk), (0, 0), (0, 0))).reshape(b, nb + 2, blk, hkv, d)
        return jnp.concatenate([tp[:, :-2], tp[:, 1:-1], tp[:, 2:]], axis=2)

    k_band, v_band = band(k), band(v)
    s_loc = jnp.einsum('bnqhgd,bnkhd->bnhgqk', qg, k_band, preferred_element_type=F32) * scale
    qpos = jnp.arange(nb)[:, None] * blk + jnp.arange(blk)[None, :]
    kpos = (jnp.arange(nb)[:, None] - 1) * blk + jnp.arange(3 * blk)[None, :]
    valid = ((jnp.abs(kpos[:, None, :] - qpos[:, :, None]) <= SWA_WINDOW)
             & (kpos[:, None, :] >= 0) & (kpos[:, None, :] < n))
    s_loc = jnp.where(valid[None, :, None, None], s_loc, NEG_INF)
    s_ctx = jnp.einsum('bnqhgd,bkhd->bnhgqk', qg, k_ctx, preferred_element_type=F32) * scale
    s_sink = jnp.broadcast_to(sink.astype(F32).reshape(hkv, grp)[:, :, None, None], s_loc.shape[:-1] + (1,))
    p = jax.nn.softmax(jnp.concatenate([s_loc, s_ctx, s_sink], axis=-1), axis=-1).astype(v.dtype)
    nk = 3 * blk
    n_ctx = k_ctx.shape[1]
    o = (jnp.einsum('bnhgqk,bnkhd->bnqhgd', p[..., :nk], v_band)
         + jnp.einsum('bnhgqk,bkhd->bnqhgd', p[..., nk:nk + n_ctx], v_ctx))
    return o.reshape(b, n, hq * d)


def ctx_attention(q, k, v, sink):
    b, nq, hq, d = q.shape
    hkv = k.shape[2]
    grp = hq // hkv
    nk = k.shape[1]
    qg = q.reshape(b, nq, hkv, grp, d)
    s = jnp.einsum('bqhgd,bkhd->bhgqk', qg, k, preferred_element_type=F32) * (d ** -0.5)
    if sink is not None:
        s_sink = jnp.broadcast_to(sink.astype(F32).reshape(hkv, grp)[None, :, :, None, None], s.shape[:-1] + (1,))
        s = jnp.concatenate([s, s_sink], axis=-1)
    p = jax.nn.softmax(s, axis=-1).astype(v.dtype)[..., :nk]
    o = jnp.einsum('bhgqk,bkhd->bqhgd', p, v)
    return o.reshape(b, nq, hq * d)


def na_project(p, q_norm, k_norm):
    b, n, _ = p.shape
    q, k, v = jnp.split(p, 3, axis=-1)
    q = rms_norm(q.reshape(b, n, NA_HEADS, NA_HEAD_DIM), q_norm)
    k = rms_norm(k.reshape(b, n, NA_HEADS, NA_HEAD_DIM), k_norm)
    v = v.reshape(b, n, NA_HEADS, NA_HEAD_DIM)
    return q, k, v


def na_latent(q, k, v, k_ctx, v_ctx, rpb):
    b, n, h, d = q.shape
    rows = n // GRID_W
    kh = min(NA_KH, rows)
    ncb = GRID_W // NA_QC
    nk = kh * NA_KC
    r = jnp.arange(rows)
    row_idx = jnp.clip(r - kh // 2, 0, rows - kh)[:, None] + jnp.arange(kh)[None, :]
    cb = jnp.arange(ncb)
    col_idx = jnp.clip(cb * NA_QC - NA_KW // 2, 0, GRID_W - NA_KC)[:, None] + jnp.arange(NA_KC)[None, :]
    q_col = cb[:, None] * NA_QC + jnp.arange(NA_QC)[None, :]
    q_col0 = jnp.clip(q_col - NA_KW // 2, 0, GRID_W - NA_KW)
    col_ok = (col_idx[:, None, :] >= q_col0[:, :, None]) & (col_idx[:, None, :] < q_col0[:, :, None] + NA_KW)
    mask = jnp.broadcast_to(col_ok[:, :, None, :], (ncb, NA_QC, kh, NA_KC)).reshape(ncb, 1, NA_QC, nk)

    def gather(t):
        grid = t.reshape(b, rows, GRID_W, h, d)
        g = grid[:, row_idx[:, None, :, None], col_idx[None, :, None, :]]
        return g.reshape(b, rows, ncb, nk, h, d)

    k_nb, v_nb = gather(k), gather(v)
    q_blk = q.reshape(b, rows, ncb, NA_QC, h, d)
    scale = d ** -0.5
    dr = row_idx - r[:, None] + (NA_KH - 1)
    dc = jnp.clip(col_idx[:, None, :] - q_col[:, :, None], 1 - NA_KW, NA_KW - 1) + (NA_KW - 1)
    bias = rpb.astype(F32)[:, dr[:, None, None, :, None], dc[None, :, :, None, :]]
    bias = jnp.moveaxis(bias, 0, 2).reshape(rows, ncb, h, NA_QC, nk)
    s_loc = jnp.einsum('brcqhd,brckhd->brchqk', q_blk, k_nb, preferred_element_type=F32) * scale + bias
    s_loc = jnp.where(mask[None, None], s_loc, NEG_INF)
    s_ctx = jnp.einsum('brcqhd,bkhd->brchqk', q_blk, k_ctx, preferred_element_type=F32) * scale
    p = jax.nn.softmax(jnp.concatenate([s_loc, s_ctx], axis=-1), axis=-1).astype(v.dtype)
    o = (jnp.einsum('brchqk,brckhd->brcqhd', p[..., :nk], v_nb)
         + jnp.einsum('brchqk,bkhd->brcqhd', p[..., nk:], v_ctx))
    return o.reshape(b, n, h * d)


def conv_ffn(h, w_up, conv_w, conv_b, w_down):
    a = h @ w_up
    ap = jnp.pad(a, ((0, 0), (1, 1), (0, 0)))
    a = ap[:, :-2] * conv_w[0] + ap[:, 1:-1] * conv_w[1] + ap[:, 2:] * conv_w[2] + conv_b
    gate, val = jnp.split(a, 2, axis=-1)
    return (jax.nn.silu(gate) * val) @ w_down


def setup_inputs(seed: int = 0) -> dict:
    key = jax.random.key(seed)
    ks = jax.random.split(key, 32)
    L = DEPTH

    def nrm(k, shape, scale):
        return jax.random.normal(k, shape, F32) * scale

    def gain(k, shape):
        return 1.0 + 0.1 * jax.random.normal(k, shape, F32)

    return {
        'x': nrm(ks[0], (BATCH, SEQ, D_MODEL), 1.0),
        'c': nrm(ks[1], (BATCH, D_MODEL), 1.0),
        'ctx': nrm(ks[2], (BATCH, CTX_LEN, D_MODEL), 1.0),
        'c_ctx': nrm(ks[3], (D_MODEL,), 1.0),
        'w_mod': nrm(ks[4], (L, D_MODEL, 6 * D_MODEL), 0.5 * D_MODEL ** -0.5),
        'b_mod': nrm(ks[5], (L, 6 * D_MODEL), 0.01),
        'g_mix': gain(ks[6], (L, D_MODEL)),
        'g_ffn': gain(ks[7], (L, D_MODEL)),
        'w_in': nrm(ks[8], (L, D_MODEL, IN_COLS), D_MODEL ** -0.5),
        'w_out': nrm(ks[9], (L, MIX_OUT, D_MODEL), MIX_OUT ** -0.5),
        'mla_q_a_norm': gain(ks[10], (L, MLA_Q_LORA)),
        'mla_w_qb': nrm(ks[11], (L, MLA_Q_LORA, MLA_HEADS * (MLA_NOPE + MLA_ROPE)), MLA_Q_LORA ** -0.5),
        'mla_kv_a_norm': gain(ks[12], (L, MLA_KV_LORA)),
        'mla_w_kvb': nrm(ks[13], (L, MLA_KV_LORA, MLA_HEADS * (MLA_NOPE + MLA_V)), MLA_KV_LORA ** -0.5),
        'mla_q_nope_norm': gain(ks[14], (L, MLA_NOPE)),
        'mla_q_rope_norm': gain(ks[15], (L, MLA_ROPE)),
        'mla_k_nope_norm': gain(ks[16], (L, MLA_NOPE)),
        'mla_k_rope_norm': gain(ks[17], (L, MLA_ROPE)),
        'pool_w': nrm(ks[18], (L, len(POOL_WINDOWS), POOL_GROUP, POOL_GROUP), POOL_GROUP ** -0.5),
        'pool_scale': gain(ks[19], (L, POOL_WIDTH)),
        'swa_q_norm': gain(ks[20], (L, SWA_HEAD_DIM)),
        'swa_k_norm': gain(ks[21], (L, SWA_HEAD_DIM)),
        'swa_sink': nrm(ks[22], (L, SWA_HEADS), 0.5),
        'na_q_norm': gain(ks[23], (L, NA_HEAD_DIM)),
        'na_k_norm': gain(ks[24], (L, NA_HEAD_DIM)),
        'na_rpb': nrm(ks[25], (L, NA_HEADS, 2 * NA_KH - 1, 2 * NA_KW - 1), 0.5),
        'ffn_w_up': nrm(ks[26], (L, D_MODEL, 2 * FFN_DIM), D_MODEL ** -0.5),
        'ffn_conv_w': nrm(ks[27], (L, 3, 2 * FFN_DIM), 3 ** -0.5),
        'ffn_conv_b': nrm(ks[28], (L, 2 * FFN_DIM), 0.02),
        'ffn_w_down': nrm(ks[29], (L, FFN_DIM, D_MODEL), FFN_DIM ** -0.5),
    }


def reference(x, c, ctx, c_ctx, w_mod, b_mod, g_mix, g_ffn, w_in, w_out,
              mla_q_a_norm, mla_w_qb, mla_kv_a_norm, mla_w_kvb,
              mla_q_nope_norm, mla_q_rope_norm, mla_k_nope_norm, mla_k_rope_norm,
              pool_w, pool_scale, swa_q_norm, swa_k_norm, swa_sink,
              na_q_norm, na_k_norm, na_rpb,
              ffn_w_up, ffn_conv_w, ffn_conv_b, ffn_w_down):
    b, n, _ = x.shape
    n_ctx = ctx.shape[1]
    ang_mla = axial_angles(n, MLA_ROPE)
    ang_swa = axial_angles(n, SWA_HEAD_DIM)
    cond_x = c[:, None, :]
    for l in range(DEPTH):
        update_ctx = l < DEPTH - 1
        sh_m, sc_m, gt_m, sh_f, sc_f, gt_f = adaln(cond_x, w_mod[l], b_mod[l])
        csh_m, csc_m, cgt_m, csh_f, csc_f, cgt_f = adaln(c_ctx, w_mod[l], b_mod[l])

        px = modulate(rms_norm(x, g_mix[l]), sh_m, sc_m) @ w_in[l]
        pc = modulate(rms_norm(ctx, g_mix[l]), csh_m, csc_m) @ w_in[l]
        mla_x, pool_x, swa_x, na_x = jnp.split(px, IN_SPLITS, axis=-1)
        mla_c, pool_c, swa_c, na_c = jnp.split(pc, IN_SPLITS, axis=-1)

        mla_w = (mla_q_a_norm[l], mla_w_qb[l], mla_kv_a_norm[l], mla_w_kvb[l],
                 mla_q_nope_norm[l], mla_q_rope_norm[l], mla_k_nope_norm[l], mla_k_rope_norm[l])
        aq_n, aq_r, ak_n, ak_r, av = mla_project(mla_x, *mla_w, ang_mla)
        cq_n, cq_r, ck_n, ck_r, cv = mla_project(mla_c, *mla_w, None)
        attend_all = functools.partial(
            mla_attend,
            k_nope=jnp.concatenate([ak_n, ck_n], axis=1),
            k_rope=jnp.concatenate([ak_r, ck_r], axis=1),
            v=jnp.concatenate([av, cv], axis=1))
        out_a = map_query_blocks(attend_all, (aq_n, aq_r)).reshape(b, n, MLA_HEADS * MLA_V)

        out_b = pool_mixer(pool_x, pool_w[l], pool_scale[l])

        sq, sk, sv = swa_project(swa_x, swa_q_norm[l], swa_k_norm[l], ang_swa)
        csq, csk, csv = swa_project(swa_c, swa_q_norm[l], swa_k_norm[l], None)
        out_c = swa_latent(sq, sk, sv, csk, csv, swa_sink[l])

        nq, nkk, nv = na_project(na_x, na_q_norm[l], na_k_norm[l])
        cnq, cnk, cnv = na_project(na_c, na_q_norm[l], na_k_norm[l])
        out_d = na_latent(nq, nkk, nv, cnk, cnv, na_rpb[l])

        if update_ctx:
            mix_c = jnp.concatenate([
                mla_attend(cq_n, cq_r, ck_n, ck_r, cv).reshape(b, n_ctx, MLA_HEADS * MLA_V),
                pool_mixer(pool_c, pool_w[l], pool_scale[l]),
                ctx_attention(csq, csk, csv, swa_sink[l]),
                ctx_attention(cnq, cnk, cnv, None),
            ], axis=-1)
            ctx = ctx + cgt_m * (mix_c @ w_out[l])
            ctx = ctx + cgt_f * conv_ffn(modulate(rms_norm(ctx, g_ffn[l]), csh_f, csc_f),
                                         ffn_w_up[l], ffn_conv_w[l], ffn_conv_b[l], ffn_w_down[l])

        mix_x = jnp.concatenate([out_a, out_b, out_c, out_d], axis=-1)
        x = x + gt_m * (mix_x @ w_out[l])
        x = x + gt_f * conv_ffn(modulate(rms_norm(x, g_ffn[l]), sh_f, sc_f),
                                ffn_w_up[l], ffn_conv_w[l], ffn_conv_b[l], ffn_w_down[l])
    return x
```

```python
import functools

import jax
import jax.numpy as jnp
import numpy as np
from jax import lax
from jax.experimental import pallas as pl
from jax.experimental.pallas import tpu as pltpu

F32 = jnp.float32
BF16 = jnp.bfloat16

GRID_W = 64
ROPE_BASE = 10000.0
NORM_EPS = 1e-6
NEG_INF = -1e30

MLA_HEADS = 4
MLA_NOPE = 128
MLA_ROPE = 64
MLA_V = 128
MLA_Q_LORA = 512
MLA_KV_LORA = 256
MLA_SCALE = (MLA_NOPE + MLA_ROPE) ** -0.5
MLA_QK = 256
POOL_WINDOWS = (2, 4, 8, 16)
POOL_GROUP = 128
POOL_WIDTH = POOL_GROUP * len(POOL_WINDOWS)
POOL_HALO = 16
SWA_HEADS = 8
SWA_KV_HEADS = 2
SWA_HEAD_DIM = 64
SWA_WINDOW = 128
SWA_BLOCK = 128
NA_HEADS = 8
NA_HEAD_DIM = 64
NA_KH = 8
NA_KW = 16
GROUP_OUT = 512

LANES = 128
PX_CQ, PX_POOL, PX_SWA_Q, PX_NA_Q, PX_NA_K, PX_NA_V = 0, 512, 1024, 1536, 2048, 2560
PX_CKV, PX_SWA_K, PX_SWA_V, PX_KR = 3072, 3328, 3456, 3584
PX_COLS = 3840

TM = 512
TP = 256
TF = 512
FFN_HALO = 16
MOD_TN = 512
VMEM_LIMIT = 52 * 1024 * 1024


def _cp(*sem, vmem=VMEM_LIMIT):
    return pltpu.CompilerParams(dimension_semantics=sem, vmem_limit_bytes=vmem)


def _resident(block, index_map):
    return pl.BlockSpec(block, index_map, pipeline_mode=pl.Buffered(1))


def _rms(x, g):
    ms = jnp.mean(x * x, axis=-1, keepdims=True)
    return x * lax.rsqrt(ms + NORM_EPS) * g


def _dot(a, b):
    return jnp.dot(a, b, preferred_element_type=F32)


def _dot_nt(a, b):
    return lax.dot_general(a, b, (((1,), (1,)), ((), ())), preferred_element_type=F32)


def _seg_mean_sq(x, seg_ones, seg):
    x2 = x * x
    hi = x2.astype(BF16)
    lo = (x2 - hi.astype(F32)).astype(BF16)
    return (_dot(hi, seg_ones) + _dot(lo, seg_ones)) * (1.0 / seg)


def _rope128(x, cos, sin_a, sin_b):
    return x * cos + pltpu.roll(x, LANES - 16, 1) * sin_a + pltpu.roll(x, 16, 1) * sin_b


def _rope(x, cos, sin_a, sin_b):
    parts = [_rope128(x[:, c:c + LANES], cos, sin_a, sin_b) for c in range(0, x.shape[1], LANES)]
    return parts[0] if len(parts) == 1 else jnp.concatenate(parts, axis=1)


def _lane_lt64(shape):
    return lax.broadcasted_iota(jnp.int32, shape, len(shape) - 1) < 64


def _mod_kernel(cond_ref, w_ref, b_ref, o_ref, sb_ref, *, n_rows):
    @pl.when((pl.program_id(0) == 0) & (pl.program_id(1) == 0))
    def _():
        c = cond_ref[...]
        s = c / (1.0 + jnp.exp(-c))
        for r in range(n_rows):
            sb_ref[r] = jnp.broadcast_to(s[:, r:r + 1], sb_ref.shape[1:])

    tn = w_ref.shape[1]
    rows = []
    for r in range(n_rows):
        parts = []
        for c in range(0, tn, LANES):
            parts.append(jnp.sum(w_ref[:, c:c + LANES] * sb_ref[r], axis=0, keepdims=True))
        rows.append(jnp.concatenate(parts, axis=1))
    rows.append(jnp.zeros((8 - n_rows, tn), F32))
    o_ref[...] = jnp.concatenate(rows, axis=0) + b_ref[...]


def _modulation(cond_t, w_mod, b_mod, n_rows):
    depth, d, n6 = w_mod.shape
    return pl.pallas_call(
        functools.partial(_mod_kernel, n_rows=n_rows),
        grid=(depth, n6 // MOD_TN),
        in_specs=[
            pl.BlockSpec((d, 8), lambda l, j: (0, 0)),
            pl.BlockSpec((None, d, MOD_TN), lambda l, j: (l, 0, j)),
            pl.BlockSpec((None, 1, MOD_TN), lambda l, j: (l, 0, j)),
        ],
        out_specs=pl.BlockSpec((None, 8, MOD_TN), lambda l, j: (l, 0, j)),
        out_shape=jax.ShapeDtypeStruct((depth, 8, n6), F32),
        scratch_shapes=[pltpu.VMEM((n_rows, d, LANES), F32)],
        compiler_params=_cp("arbitrary", "arbitrary"),
        name="adaln_modulation",
    )(cond_t, w_mod, b_mod.reshape(depth, 1, n6))


class _Layout:
    def __init__(self, batch, n, n_ctx):
        self.batch, self.n, self.n_ctx = batch, n, n_ctx
        self.n_lat = batch * n
        self.total = batch * (n + n_ctx)
        self.nk = n + n_ctx
        assert n % TM == 0 and (batch * n_ctx) % TM == 0 and n_ctx % TP == 0 and n % GRID_W == 0
        assert n // GRID_W >= NA_KH and n_ctx % SWA_BLOCK == 0 and n >= 3 * SWA_BLOCK

    def mod_row(self, tile, tile_rows):
        lat_tiles = self.n_lat // tile_rows
        return jnp.where(tile < lat_tiles, tile // (self.n // tile_rows), self.batch)

    def key_block(self, tile, tile_rows):
        lat_tiles = self.n_lat // tile_rows
        per_b = self.n // tile_rows
        per_c = self.n_ctx // tile_rows
        c = tile - lat_tiles
        b = jnp.where(tile < lat_tiles, tile // per_b, c // per_c)
        blk = jnp.where(tile < lat_tiles, tile % per_b, per_b + c % per_c)
        return b, blk

    def row_block(self, b, blk, tile_rows):
        per_b = self.n // tile_rows
        per_c = self.n_ctx // tile_rows
        return jnp.where(blk < per_b, b * per_b + blk, self.batch * per_b + b * per_c + (blk - per_b))


def _inproj_kernel(x_ref, g_ref, sh_ref, sc_ref, w_ref, o_ref):
    h = _rms(x_ref[...], g_ref[...]) * (1.0 + sc_ref[...]) + sh_ref[...]
    o_ref[...] = _dot(h.astype(BF16), w_ref[...])


def _in_projection(lay, l, xt, g_mix, mod4, w_in_p):
    d = xt.shape[1]
    row = lambda i: lay.mod_row(i, TM)
    return pl.pallas_call(
        _inproj_kernel,
        grid=(lay.total // TM,),
        in_specs=[
            pl.BlockSpec((TM, d), lambda i: (i, 0)),
            pl.BlockSpec((None, 1, d), lambda i: (l, 0, 0)),
            pl.BlockSpec((None, None, 1, d), lambda i: (l, row(i), 0, 0)),
            pl.BlockSpec((None, None, 1, d), lambda i: (l, row(i), 0, 1)),
            _resident((None, d, PX_COLS), lambda i: (l, 0, 0)),
        ],
        out_specs=pl.BlockSpec((TM, PX_COLS), lambda i: (i, 0)),
        out_shape=jax.ShapeDtypeStruct((lay.total, PX_COLS), F32),
        compiler_params=_cp("arbitrary"),
        name="in_projection",
    )(xt, g_mix, mod4, mod4, w_in_p)


def _mla_proj_kernel(cq_ref, ckv_ref, kr_ref, cos_ref, sa_ref, sb_ref, seg_ref,
                     qan_ref, wqb_ref, kvan_ref, wkvb_ref, qnn_ref, qrn_ref, knn_ref, krn_ref,
                     q_out, k_out, v_out):
    cos, sa, sb = cos_ref[...], sa_ref[...], sb_ref[...]
    q = _dot(_rms(cq_ref[...], qan_ref[...]).astype(BF16), wqb_ref[...])
    kv = _dot(_rms(ckv_ref[...], kvan_ref[...]).astype(BF16), wkvb_ref[...])

    kr = kr_ref[...]
    kr = kr * lax.rsqrt(jnp.sum(kr * kr, axis=-1, keepdims=True) * (1.0 / MLA_ROPE) + NORM_EPS) * krn_ref[...]
    kr = _rope128(kr, cos, sa, sb).astype(BF16)

    nope_w = MLA_HEADS * MLA_NOPE
    qr = q[:, nope_w:]
    qr = qr * lax.rsqrt(_seg_mean_sq(qr, seg_ref[...], MLA_ROPE) + NORM_EPS) * qrn_ref[...]
    qr = _rope(qr, cos, sa, sb)
    lo = _lane_lt64((qr.shape[0], LANES))
    for h in range(MLA_HEADS):
        q_out[h, :, 0:MLA_NOPE] = _rms(q[:, h * MLA_NOPE:(h + 1) * MLA_NOPE], qnn_ref[...]).astype(BF16)
        pair = qr[:, (h // 2) * LANES:(h // 2 + 1) * LANES]
        if h % 2:
            pair = pltpu.roll(pair, 64, 1)
        q_out[h, :, MLA_NOPE:] = jnp.where(lo, pair, 0.0).astype(BF16)
        base = h * (MLA_NOPE + MLA_V)
        k_out[h, :, 0:MLA_NOPE] = _rms(kv[:, base:base + MLA_NOPE], knn_ref[...]).astype(BF16)
        k_out[h, :, MLA_NOPE:] = kr
        v_out[h] = kv[:, base + MLA_NOPE:base + MLA_NOPE + MLA_V].astype(BF16)


def _mla_projection(lay, l, px, rope_tabs, seg256, w):
    cos_t, sa_t, sb_t = rope_tabs
    kb = lambda i: lay.key_block(i, TP)
    tab = pl.BlockSpec((TP, LANES), lambda i: (kb(i)[1], 0))
    vec = lambda width: pl.BlockSpec((None, 1, width), lambda i: (l, 0, 0))
    head_out = lambda width: pl.BlockSpec((None, MLA_HEADS, TP, width), lambda i: (kb(i)[0], 0, kb(i)[1], 0))
    shp = lambda width: jax.ShapeDtypeStruct((lay.batch, MLA_HEADS, lay.nk, width), BF16)
    return pl.pallas_call(
        _mla_proj_kernel,
        grid=(lay.total // TP,),
        in_specs=[
            pl.BlockSpec((TP, MLA_Q_LORA), lambda i: (i, PX_CQ // MLA_Q_LORA)),
            pl.BlockSpec((TP, MLA_KV_LORA), lambda i: (i, PX_CKV // MLA_KV_LORA)),
            pl.BlockSpec((TP, LANES), lambda i: (i, PX_KR // LANES)),
            tab, tab, tab,
            _resident((256, 256), lambda i: (0, 0)),
            vec(MLA_Q_LORA),
            _resident((None, MLA_Q_LORA, w["mla_w_qb"].shape[2]), lambda i: (l, 0, 0)),
            vec(MLA_KV_LORA),
            _resident((None, MLA_KV_LORA, w["mla_w_kvb"].shape[2]), lambda i: (l, 0, 0)),
            vec(MLA_NOPE), vec(MLA_HEADS * MLA_ROPE), vec(MLA_NOPE), vec(LANES),
        ],
        out_specs=[head_out(MLA_QK), head_out(MLA_QK), head_out(MLA_V)],
        out_shape=[shp(MLA_QK), shp(MLA_QK), shp(MLA_V)],
        compiler_params=_cp("arbitrary"),
        name="mla_projection",
    )(px, px, px, cos_t, sa_t, sb_t, seg256,
      w["mla_q_a_norm"], w["mla_w_qb"], w["mla_kv_a_norm"], w["mla_w_kvb"],
      w["mla_q_nope_norm"], w["mla_q_rope_norm"], w["mla_k_nope_norm"], w["mla_k_rope_norm"])


def _mla_attn_kernel(q_ref, k_ref, v_ref, o_ref, *, n, lat_tiles):
    q = q_ref[...]

    def attend(k, v):
        s = _dot_nt(q, k) * MLA_SCALE
        p = jnp.exp(s - jnp.max(s, axis=-1, keepdims=True))
        den = jnp.sum(p, axis=-1, keepdims=True)
        o_ref[...] = (_dot(p.astype(BF16), v) / den).astype(o_ref.dtype)

    @pl.when(pl.program_id(2) < lat_tiles)
    def _():
        attend(k_ref[...], v_ref[...])

    @pl.when(pl.program_id(2) >= lat_tiles)
    def _():
        attend(k_ref[n:, :], v_ref[n:, :])


def _mla_attention(lay, q, k, v, with_ctx):
    lat_tiles = lay.n // TP
    tiles = lat_tiles + (lay.n_ctx // TP if with_ctx else 0)
    return pl.pallas_call(
        functools.partial(_mla_attn_kernel, n=lay.n, lat_tiles=lat_tiles),
        grid=(lay.batch, MLA_HEADS, tiles),
        in_specs=[
            pl.BlockSpec((None, None, TP, MLA_QK), lambda b, h, t: (b, h, t, 0)),
            pl.BlockSpec((None, None, lay.nk, MLA_QK), lambda b, h, t: (b, h, 0, 0)),
            pl.BlockSpec((None, None, lay.nk, MLA_V), lambda b, h, t: (b, h, 0, 0)),
        ],
        out_specs=pl.BlockSpec((TP, MLA_V), lambda b, h, t: (lay.row_block(b, t, TP), h)),
        out_shape=jax.ShapeDtypeStruct((lay.total, GROUP_OUT), BF16),
        compiler_params=_cp("arbitrary", "arbitrary", "arbitrary"),
        name="mla_attention",
    )(q, k, v)


def _pool_kernel(*refs, n, aliased):
    u_ref, w_ref, scale_ref = refs[:3]
    o_ref, pad_ref = refs[-2:]
    del aliased
    zeros = jnp.zeros((POOL_HALO, POOL_WIDTH), F32)
    pad_ref[0:POOL_HALO, :] = zeros
    pad_ref[POOL_HALO + n:, :] = zeros
    pad_ref[POOL_HALO:POOL_HALO + n, :] = u_ref[...]
    t = lax.broadcasted_iota(jnp.int32, (n, 1), 0)
    for g, win in enumerate(POOL_WINDOWS):
        cols = slice(g * POOL_GROUP, (g + 1) * POOL_GROUP)
        half = win // 2
        tot = pad_ref[POOL_HALO - half:POOL_HALO - half + n, cols]
        for off in range(1 - half, half):
            tot = tot + pad_ref[POOL_HALO + off:POOL_HALO + off + n, cols]
        cnt = (jnp.minimum(t + half, n) - jnp.maximum(t - half, 0)).astype(F32)
        diff = tot / cnt - u_ref[:, cols]
        y = _dot(diff.astype(BF16), w_ref[g]) * scale_ref[:, cols]
        o_ref[:, cols] = y.astype(o_ref.dtype)


def _pool_mixer(lay, l, px, pool_w, pool_scale, n, first_block, prev=None):
    args = [px, pool_w, pool_scale]
    in_specs = [
        pl.BlockSpec((n, POOL_WIDTH), lambda b: (first_block + b, PX_POOL // POOL_WIDTH)),
        pl.BlockSpec((None, len(POOL_WINDOWS), POOL_GROUP, POOL_GROUP), lambda b: (l, 0, 0, 0)),
        pl.BlockSpec((None, 1, POOL_WIDTH), lambda b: (l, 0, 0)),
    ]
    aliases = {}
    if prev is not None:
        args.append(prev)
        in_specs.append(pl.BlockSpec(memory_space=pl.ANY))
        aliases = {3: 0}
    return pl.pallas_call(
        functools.partial(_pool_kernel, n=n, aliased=prev is not None),
        grid=(lay.batch,),
        in_specs=in_specs,
        out_specs=pl.BlockSpec((n, POOL_WIDTH), lambda b: (first_block + b, 0)),
        out_shape=jax.ShapeDtypeStruct((lay.total, GROUP_OUT), BF16),
        scratch_shapes=[pltpu.VMEM((n + 2 * POOL_HALO, POOL_WIDTH), F32)],
        input_output_aliases=aliases,
        compiler_params=_cp("arbitrary"),
        name="pool_mixer",
    )(*args)


def _swa_proj_kernel(q_ref, k_ref, v_ref, cos_ref, sa_ref, sb_ref, seg_ref, qn_ref, kn_ref,
                     q_out, k_out, v_out):
    cos, sa, sb = cos_ref[...], sa_ref[...], sb_ref[...]
    scale = SWA_HEAD_DIM ** -0.5
    q = q_ref[...]
    q = q * lax.rsqrt(_seg_mean_sq(q, seg_ref[...], SWA_HEAD_DIM) + NORM_EPS) * qn_ref[...]
    q_out[...] = (_rope(q, cos, sa, sb) * scale).astype(BF16)
    k = k_ref[...]
    k = k * lax.rsqrt(_seg_mean_sq(k, seg_ref[0:LANES, 0:LANES], SWA_HEAD_DIM) + NORM_EPS) * kn_ref[...]
    k = _rope128(k, cos, sa, sb)
    v = v_ref[...]
    lo = _lane_lt64(k.shape)
    k_sw, v_sw = pltpu.roll(k, 64, 1), pltpu.roll(v, 64, 1)
    k_out[0] = jnp.where(lo, k, k_sw).astype(BF16)
    k_out[1] = jnp.where(lo, k_sw, k).astype(BF16)
    v_out[0] = jnp.where(lo, v, v_sw).astype(BF16)
    v_out[1] = jnp.where(lo, v_sw, v).astype(BF16)


def _swa_projection(lay, l, px, rope_tabs, seg512, q_norm_t, k_norm_t):
    cos_t, sa_t, sb_t = rope_tabs
    kb = lambda i: lay.key_block(i, TP)
    tab = pl.BlockSpec((TP, LANES), lambda i: (kb(i)[1], 0))
    qw = SWA_HEADS * SWA_HEAD_DIM
    kv_out = pl.BlockSpec((None, SWA_KV_HEADS, TP, LANES), lambda i: (kb(i)[0], 0, kb(i)[1], 0))
    kv_shape = jax.ShapeDtypeStruct((lay.batch, SWA_KV_HEADS, lay.nk, LANES), BF16)
    return pl.pallas_call(
        _swa_proj_kernel,
        grid=(lay.total // TP,),
        in_specs=[
            pl.BlockSpec((TP, qw), lambda i: (i, PX_SWA_Q // qw)),
            pl.BlockSpec((TP, LANES), lambda i: (i, PX_SWA_K // LANES)),
            pl.BlockSpec((TP, LANES), lambda i: (i, PX_SWA_V // LANES)),
            tab, tab, tab,
            _resident((512, 512), lambda i: (0, 0)),
            pl.BlockSpec((None, 1, qw), lambda i: (l, 0, 0)),
            pl.BlockSpec((None, 1, LANES), lambda i: (l, 0, 0)),
        ],
        out_specs=[pl.BlockSpec((TP, qw), lambda i: (i, 0)), kv_out, kv_out],
        out_shape=[jax.ShapeDtypeStruct((lay.total, qw), BF16), kv_shape, kv_shape],
        compiler_params=_cp("arbitrary"),
        name="swa_projection",
    )(px, px, px, cos_t, sa_t, sb_t, seg512, q_norm_t, k_norm_t)


def _swa_attn_kernel(sink_ref, q_ref, k_ref, v_ref, o_ref, *, n, n_ctx, lat_blocks):
    blk = pl.program_id(1)
    q = q_ref[...]
    rows = q.shape[0]
    lo = _lane_lt64((rows, LANES))
    grp = SWA_HEADS // SWA_KV_HEADS
    zero = jnp.zeros((), q.dtype)

    def stacked_queries(g):
        parts = []
        for pair in range(grp // 2):
            c = (g * grp // 2 + pair) * LANES
            qp = q[:, c:c + LANES]
            parts += [jnp.where(lo, qp, zero), jnp.where(lo, zero, qp)]
        return jnp.concatenate(parts, axis=0)

    def sink_col(g):
        parts = [jnp.full((rows, 1), sink_ref[g * grp + h], F32) for h in range(grp)]
        return jnp.concatenate(parts, axis=0)

    def finish(g, o):
        for pair in range(grp // 2):
            even = o[(2 * pair) * rows:(2 * pair + 1) * rows]
            odd = o[(2 * pair + 1) * rows:(2 * pair + 2) * rows]
            c = (g * grp // 2 + pair) * LANES
            o_ref[:, c:c + LANES] = jnp.where(lo, even, odd).astype(o_ref.dtype)

    @pl.when(blk < lat_blocks)
    def _():
        start = pl.multiple_of(jnp.clip(blk - 1, 0, lat_blocks - 3) * SWA_BLOCK, SWA_BLOCK)
        band = 3 * SWA_BLOCK
        qpos = blk * SWA_BLOCK + (lax.broadcasted_iota(jnp.int32, (grp * rows, band), 0) & (rows - 1))
        kpos = start + lax.broadcasted_iota(jnp.int32, (grp * rows, band), 1)
        valid = jnp.abs(kpos - qpos) <= SWA_WINDOW
        for g in range(SWA_KV_HEADS):
            qs = stacked_queries(g)
            s_loc = jnp.where(valid, _dot_nt(qs, k_ref[g, pl.ds(start, band), :]), NEG_INF)
            s_ctx = _dot_nt(qs, k_ref[g, n:, :])
            snk = sink_col(g)
            m = jnp.maximum(jnp.maximum(jnp.max(s_loc, axis=-1, keepdims=True),
                                        jnp.max(s_ctx, axis=-1, keepdims=True)), snk)
            p_loc, p_ctx = jnp.exp(s_loc - m), jnp.exp(s_ctx - m)
            den = (jnp.sum(p_loc, axis=-1, keepdims=True) + jnp.sum(p_ctx, axis=-1, keepdims=True)
                   + jnp.exp(snk - m))
            o = _dot(p_loc.astype(BF16), v_ref[g, pl.ds(start, band), :]) + _dot(p_ctx.astype(BF16), v_ref[g, n:, :])
            finish(g, o / den)

    @pl.when(blk >= lat_blocks)
    def _():
        for g in range(SWA_KV_HEADS):
            qs = stacked_queries(g)
            s_ctx = _dot_nt(qs, k_ref[g, n:, :])
            snk = sink_col(g)
            m = jnp.maximum(jnp.max(s_ctx, axis=-1, keepdims=True), snk)
            p_ctx = jnp.exp(s_ctx - m)
            den = jnp.sum(p_ctx, axis=-1, keepdims=True) + jnp.exp(snk - m)
            finish(g, _dot(p_ctx.astype(BF16), v_ref[g, n:, :]) / den)


def _swa_attention(lay, l, q, k, v, sink, with_ctx):
    lat_blocks = lay.n // SWA_BLOCK
    blocks = lat_blocks + (lay.n_ctx // SWA_BLOCK if with_ctx else 0)
    qw = SWA_HEADS * SWA_HEAD_DIM
    kv_spec = pl.BlockSpec((None, SWA_KV_HEADS, lay.nk, LANES), lambda b, t: (b, 0, 0, 0))
    return pl.pallas_call(
        functools.partial(_swa_attn_kernel, n=lay.n, n_ctx=lay.n_ctx, lat_blocks=lat_blocks),
        grid=(lay.batch, blocks),
        in_specs=[
            pl.BlockSpec(memory_space=pltpu.SMEM),
            pl.BlockSpec((SWA_BLOCK, qw), lambda b, t: (lay.row_block(b, t, SWA_BLOCK), 0)),
            kv_spec, kv_spec,
        ],
        out_specs=pl.BlockSpec((SWA_BLOCK, qw), lambda b, t: (lay.row_block(b, t, SWA_BLOCK), 0)),
        out_shape=jax.ShapeDtypeStruct((lay.total, GROUP_OUT), BF16),
        compiler_params=_cp("arbitrary", "arbitrary"),
        name="swa_attention",
    )(sink[l], q, k, v)


def _na_proj_kernel(q_ref, k_ref, v_ref, seg_ref, qn_ref, kn_ref, q_out, k_out, v_out):
    scale = NA_HEAD_DIM ** -0.5
    seg = seg_ref[...]
    q = q_ref[...]
    q_out[...] = (q * lax.rsqrt(_seg_mean_sq(q, seg, NA_HEAD_DIM) + NORM_EPS) * (qn_ref[...] * scale)).astype(BF16)
    k = k_ref[...]
    k_out[...] = (k * lax.rsqrt(_seg_mean_sq(k, seg, NA_HEAD_DIM) + NORM_EPS) * kn_ref[...]).astype(BF16)
    v_out[...] = v_ref[...].astype(BF16)


def _na_projection(lay, l, px, seg512, q_norm_t, k_norm_t):
    w = NA_HEADS * NA_HEAD_DIM
    kb = lambda i: lay.key_block(i, TP)
    kv_out = pl.BlockSpec((None, TP, w), lambda i: (kb(i)[0], kb(i)[1], 0))
    kv_shape = jax.ShapeDtypeStruct((lay.batch, lay.nk, w), BF16)
    vec = pl.BlockSpec((None, 1, w), lambda i: (l, 0, 0))
    return pl.pallas_call(
        _na_proj_kernel,
        grid=(lay.total // TP,),
        in_specs=[
            pl.BlockSpec((TP, w), lambda i: (i, PX_NA_Q // w)),
            pl.BlockSpec((TP, w), lambda i: (i, PX_NA_K // w)),
            pl.BlockSpec((TP, w), lambda i: (i, PX_NA_V // w)),
            _resident((512, 512), lambda i: (0, 0)),
            vec, vec,
        ],
        out_specs=[pl.BlockSpec((TP, w), lambda i: (i, 0)), kv_out, kv_out],
        out_shape=[jax.ShapeDtypeStruct((lay.total, w), BF16), kv_shape, kv_shape],
        compiler_params=_cp("arbitrary"),
        name="na_projection",
    )(px, px, px, seg512, q_norm_t, k_norm_t)


def _na_attn_kernel(q_ref, k_ref, v_ref, bias_ref, o_ref, *, n, grid_rows):
    r = pl.program_id(1)
    q = q_ref[...]
    rows = q.shape[0]
    lo = _lane_lt64((rows, LANES))
    zero = jnp.zeros((), q.dtype)
    win = NA_KH * GRID_W

    def stacked_queries(pair):
        qp = q[:, pair * LANES:(pair + 1) * LANES]
        return jnp.concatenate([jnp.where(lo, qp, zero), jnp.where(lo, zero, qp)], axis=0)

    def finish(pair, o):
        o_ref[:, pair * LANES:(pair + 1) * LANES] = jnp.where(lo, o[:rows], o[rows:]).astype(o_ref.dtype)

    @pl.when(r < grid_rows)
    def _():
        first = jnp.clip(r - NA_KH // 2, 0, grid_rows - NA_KH)
        rel = r - first
        start = pl.multiple_of(first * GRID_W, GRID_W)
        for pair in range(NA_HEADS // 2):
            c = slice(pair * LANES, (pair + 1) * LANES)
            qs = stacked_queries(pair)
            s_loc = _dot_nt(qs, k_ref[pl.ds(start, win), c]) + bias_ref[rel, pair]
            s_ctx = _dot_nt(qs, k_ref[n:, c])
            m = jnp.maximum(jnp.max(s_loc, axis=-1, keepdims=True), jnp.max(s_ctx, axis=-1, keepdims=True))
            p_loc, p_ctx = jnp.exp(s_loc - m), jnp.exp(s_ctx - m)
            den = jnp.sum(p_loc, axis=-1, keepdims=True) + jnp.sum(p_ctx, axis=-1, keepdims=True)
            o = _dot(p_loc.astype(BF16), v_ref[pl.ds(start, win), c]) + _dot(p_ctx.astype(BF16), v_ref[n:, c])
            finish(pair, o / den)

    @pl.when(r >= grid_rows)
    def _():
        for pair in range(NA_HEADS // 2):
            c = slice(pair * LANES, (pair + 1) * LANES)
            s_ctx = _dot_nt(stacked_queries(pair), k_ref[n:, c])
            p_ctx = jnp.exp(s_ctx - jnp.max(s_ctx, axis=-1, keepdims=True))
            den = jnp.sum(p_ctx, axis=-1, keepdims=True)
            finish(pair, _dot(p_ctx.astype(BF16), v_ref[n:, c]) / den)


def _na_attention(lay, l, q, k, v, bias, with_ctx):
    grid_rows = lay.n // GRID_W
    steps = grid_rows + (lay.n_ctx // GRID_W if with_ctx else 0)
    w = NA_HEADS * NA_HEAD_DIM
    kv_spec = pl.BlockSpec((None, lay.nk, w), lambda b, t: (b, 0, 0))
    return pl.pallas_call(
        functools.partial(_na_attn_kernel, n=lay.n, grid_rows=grid_rows),
        grid=(lay.batch, steps),
        in_specs=[
            pl.BlockSpec((GRID_W, w), lambda b, t: (lay.row_block(b, t, GRID_W), 0)),
            kv_spec, kv_spec,
            _resident((None,) + bias.shape[1:], lambda b, t: (l, 0, 0, 0, 0)),
        ],
        out_specs=pl.BlockSpec((GRID_W, w), lambda b, t: (lay.row_block(b, t, GRID_W), 0)),
        out_shape=jax.ShapeDtypeStruct((lay.total, GROUP_OUT), BF16),
        compiler_params=_cp("arbitrary", "arbitrary"),
        name="na_attention",
    )(q, k, v, bias)


def _na_bias_tables(rpb):
    depth = rpb.shape[0]
    rel = np.arange(NA_KH)[:, None]
    j = np.arange(NA_KH)[None, :]
    dr = j - rel + (NA_KH - 1)
    c = np.arange(GRID_W)[:, None]
    kc = np.arange(GRID_W)[None, :]
    dc = np.clip(kc - c, 1 - NA_KW, NA_KW - 1) + (NA_KW - 1)
    c0 = np.clip(c - NA_KW // 2, 0, GRID_W - NA_KW)
    valid = (kc >= c0) & (kc < c0 + NA_KW)
    tab = rpb.astype(F32)[:, :, dr[:, :, None, None], dc[None, None, :, :]]
    tab = jnp.where(valid[None, None, None, None], tab, NEG_INF)
    tab = tab.transpose(0, 2, 1, 4, 3, 5)
    return tab.reshape(depth, NA_KH, NA_HEADS // 2, 2 * GRID_W, NA_KH * GRID_W)


def _outproj_kernel(x_ref, a_ref, b_ref, c_ref, d_ref, w_ref, gt_ref, o_ref):
    acc = _dot(a_ref[...], w_ref[0:GROUP_OUT, :])
    for g, m_ref in enumerate((b_ref, c_ref, d_ref), start=1):
        acc = acc + _dot(m_ref[...], w_ref[g * GROUP_OUT:(g + 1) * GROUP_OUT, :])
    o_ref[...] = x_ref[...] + gt_ref[...] * acc


def _out_projection(lay, l, xt, mixes, w_out, mod4, with_ctx):
    d = xt.shape[1]
    tiles = (lay.total if with_ctx else lay.n_lat) // TM
    row = lambda i: lay.mod_row(i, TM)
    mix = pl.BlockSpec((TM, GROUP_OUT), lambda i: (i, 0))
    return pl.pallas_call(
        _outproj_kernel,
        grid=(tiles,),
        in_specs=[
            pl.BlockSpec((TM, d), lambda i: (i, 0)),
            mix, mix, mix, mix,
            _resident((None, 4 * GROUP_OUT, d), lambda i: (l, 0, 0)),
            pl.BlockSpec((None, None, 1, d), lambda i: (l, row(i), 0, 2)),
        ],
        out_specs=pl.BlockSpec((TM, d), lambda i: (i, 0)),
        out_shape=jax.ShapeDtypeStruct(xt.shape, F32),
        input_output_aliases={0: 0},
        compiler_params=_cp("arbitrary"),
        name="out_projection",
    )(xt, *mixes, w_out, mod4)


def _ffn_kernel(xm_ref, xp_ref, xn_ref, g_ref, sh_ref, sc_ref, gt_ref, wg_ref, wv_ref,
                cwg_ref, cwv_ref, cbg_ref, cbv_ref, wd_ref, o_ref, h_ref, acc_ref, *, seq_of_tile, max_seqs):
    i, j = pl.program_id(0), pl.program_id(1)
    tm = xm_ref.shape[0]

    @pl.when(j == 0)
    def _():
        def norm_mod(x):
            return _rms(x, g_ref[...]) * (1.0 + sc_ref[...]) + sh_ref[...]

        h_ref[0:tm, :] = norm_mod(xm_ref[...]).astype(BF16)
        h_ref[tm:, :] = norm_mod(jnp.concatenate([xn_ref[...], xp_ref[...]], axis=0)).astype(BF16)
        acc_ref[...] = jnp.zeros_like(acc_ref)

    h = h_ref[...]
    a_gate = _dot(h, wg_ref[...])
    a_val = _dot(h, wv_ref[...])
    seq, offset = seq_of_tile(i)
    start0 = lax.rem(seq - offset, seq)
    t = lax.broadcasted_iota(jnp.int32, (tm, 1), 0)
    first, last = t == start0, t == start0 - 1
    for k in range(1, max_seqs + 1):
        first = first | (t == start0 + k * seq)
        last = last | (t == start0 + k * seq - 1)
    ext = tm + FFN_HALO

    def conv(a, cw_ref, cb_ref):
        prev = jnp.where(first, 0.0, pltpu.roll(a, 1, 0)[0:tm])
        nxt = jnp.where(last, 0.0, pltpu.roll(a, ext - 1, 0)[0:tm])
        return prev * cw_ref[0:1, :] + a[0:tm] * cw_ref[1:2, :] + nxt * cw_ref[2:3, :] + cb_ref[...]

    gate = conv(a_gate, cwg_ref, cbg_ref)
    val = conv(a_val, cwv_ref, cbv_ref)
    act = gate / (1.0 + jnp.exp(-gate)) * val
    acc_ref[...] += _dot(act.astype(BF16), wd_ref[...])

    @pl.when(j == pl.num_programs(1) - 1)
    def _():
        o_ref[...] = xm_ref[...] + gt_ref[...] * acc_ref[...]


def _conv_ffn(lay, l, xt, g_ffn, mod4, w_up, conv_w, conv_b, w_down, with_ctx):
    d = xt.shape[1]
    ffn = w_down.shape[1]
    tiles = (lay.total if with_ctx else lay.n_lat) // TM
    lat_tiles = lay.n_lat // TM
    n_steps = ffn // TF
    row = lambda i: lay.mod_row(i, TM)
    last8 = lay.total // 8 - 1
    def seq_of_tile(i):
        is_lat = i < lat_tiles
        seq = jnp.where(is_lat, lay.n, lay.n_ctx)
        return seq, lax.rem(jnp.where(is_lat, i, i - lat_tiles) * TM, seq)

    modspec = lambda k: pl.BlockSpec((None, None, 1, d), lambda i, j: (l, row(i), 0, k))
    return pl.pallas_call(
        functools.partial(_ffn_kernel, seq_of_tile=seq_of_tile, max_seqs=-(-TM // min(lay.n, lay.n_ctx))),
        grid=(tiles, n_steps),
        in_specs=[
            pl.BlockSpec((TM, d), lambda i, j: (i, 0)),
            pl.BlockSpec((8, d), lambda i, j: (jnp.maximum(i * (TM // 8) - 1, 0), 0)),
            pl.BlockSpec((8, d), lambda i, j: (jnp.minimum((i + 1) * (TM // 8), last8), 0)),
            pl.BlockSpec((None, 1, d), lambda i, j: (l, 0, 0)),
            modspec(3), modspec(4), modspec(5),
            pl.BlockSpec((None, d, TF), lambda i, j: (l, 0, j)),
            pl.BlockSpec((None, d, TF), lambda i, j: (l, 0, n_steps + j)),
            pl.BlockSpec((None, 3, TF), lambda i, j: (l, 0, j)),
            pl.BlockSpec((None, 3, TF), lambda i, j: (l, 0, n_steps + j)),
            pl.BlockSpec((None, 1, TF), lambda i, j: (l, 0, j)),
            pl.BlockSpec((None, 1, TF), lambda i, j: (l, 0, n_steps + j)),
            pl.BlockSpec((None, TF, d), lambda i, j: (l, j, 0)),
        ],
        out_specs=pl.BlockSpec((TM, d), lambda i, j: (i, 0)),
        out_shape=jax.ShapeDtypeStruct(xt.shape, F32),
        scratch_shapes=[pltpu.VMEM((TM + FFN_HALO, d), BF16), pltpu.VMEM((TM, d), F32)],
        compiler_params=_cp("arbitrary", "arbitrary"),
        name="conv_ffn",
    )(xt, xt, xt, g_ffn, mod4, mod4, mod4, w_up, w_up, conv_w, conv_w, conv_b, conv_b, w_down)


def _rope_tables(n, n_ctx, d_rot):
    t = jnp.arange(n)
    row = (t // GRID_W).astype(F32)
    col = (t % GRID_W).astype(F32)
    d_axis = d_rot // 2
    inv_freq = ROPE_BASE ** (-jnp.arange(0, d_axis, 2, dtype=F32) / d_axis)
    ang_r, ang_c = row[:, None] * inv_freq, col[:, None] * inv_freq
    cos = jnp.concatenate([jnp.cos(ang_r)] * 2 + [jnp.cos(ang_c)] * 2, axis=-1)
    sin = jnp.concatenate([jnp.sin(ang_r)] * 2 + [jnp.sin(ang_c)] * 2, axis=-1)
    first_half = (jnp.arange(d_rot) % (d_rot // 2)) < d_rot // 4
    sin_a = jnp.where(first_half, -sin, 0.0)
    sin_b = jnp.where(first_half, 0.0, sin)

    def finish(tab, ctx_value):
        tab = jnp.concatenate([tab, jnp.full((n_ctx, d_rot), ctx_value, F32)], axis=0)
        return jnp.concatenate([tab] * (LANES // d_rot), axis=-1)

    return finish(cos, 1.0), finish(sin_a, 0.0), finish(sin_b, 0.0)


def _seg_ones(width, seg):
    idx = np.arange(width) // seg
    return jnp.asarray(idx[:, None] == idx[None, :], dtype=BF16)


def kernel(x, c, ctx, c_ctx, w_mod, b_mod, g_mix, g_ffn, w_in, w_out, mla_q_a_norm, mla_w_qb, mla_kv_a_norm, mla_w_kvb, mla_q_nope_norm, mla_q_rope_norm, mla_k_nope_norm, mla_k_rope_norm, pool_w, pool_scale, swa_q_norm, swa_k_norm, swa_sink, na_q_norm, na_k_norm, na_rpb, ffn_w_up, ffn_conv_w, ffn_conv_b, ffn_w_down):
    batch, n, d = x.shape
    n_ctx = ctx.shape[1]
    depth = w_in.shape[0]
    lay = _Layout(batch, n, n_ctx)
    assert batch + 1 <= 8

    seg = lambda a, lo, hi: a[..., lo:hi]
    w_in_p = jnp.concatenate(
        [seg(w_in, 0, 512), seg(w_in, 832, 1344), seg(w_in, 1344, 1856), seg(w_in, 2112, 2624),
         seg(w_in, 2624, 3136), seg(w_in, 3136, 3648), seg(w_in, 512, 768), seg(w_in, 1856, 1984),
         seg(w_in, 1984, 2112), seg(w_in, 768, 832),
         jnp.zeros(w_in.shape[:2] + (PX_COLS - w_in.shape[2],), w_in.dtype)], axis=-1).astype(BF16)
    qk = MLA_NOPE + MLA_ROPE
    w_qb_p = jnp.concatenate(
        [seg(mla_w_qb, h * qk, h * qk + MLA_NOPE) for h in range(MLA_HEADS)]
        + [seg(mla_w_qb, h * qk + MLA_NOPE, (h + 1) * qk) for h in range(MLA_HEADS)], axis=-1).astype(BF16)
    row3 = lambda a: a.reshape(depth, 1, -1)
    tile_to = lambda a, width: row3(jnp.tile(a, (1, width // a.shape[1])))
    mla_w = {
        "mla_q_a_norm": row3(mla_q_a_norm), "mla_w_qb": w_qb_p,
        "mla_kv_a_norm": row3(mla_kv_a_norm), "mla_w_kvb": mla_w_kvb.astype(BF16),
        "mla_q_nope_norm": row3(mla_q_nope_norm),
        "mla_q_rope_norm": tile_to(mla_q_rope_norm, MLA_HEADS * MLA_ROPE),
        "mla_k_nope_norm": row3(mla_k_nope_norm),
        "mla_k_rope_norm": row3(jnp.pad(mla_k_rope_norm, ((0, 0), (0, LANES - MLA_ROPE)))),
    }
    pool_w_b = pool_w.astype(BF16)
    w_out_b = w_out.astype(BF16)
    w_up_b = ffn_w_up.astype(BF16)
    w_down_b = ffn_w_down.astype(BF16)
    swa_qn = tile_to(swa_q_norm, SWA_HEADS * SWA_HEAD_DIM)
    swa_kn = tile_to(swa_k_norm, LANES)
    na_qn = tile_to(na_q_norm, NA_HEADS * NA_HEAD_DIM)
    na_kn = tile_to(na_k_norm, NA_HEADS * NA_HEAD_DIM)
    na_bias = _na_bias_tables(na_rpb)
    rope_mla = _rope_tables(n, n_ctx, MLA_ROPE)
    rope_swa = _rope_tables(n, n_ctx, SWA_HEAD_DIM)
    seg512 = _seg_ones(512, 64)
    seg256 = _seg_ones(256, 64)

    cond_t = jnp.zeros((d, 8), F32).at[:, :batch].set(c.T).at[:, batch].set(c_ctx)
    mod = _modulation(cond_t, w_mod, b_mod, batch + 1)
    mod4 = mod.reshape(depth, 8, 1, 6 * d)

    xt = jnp.concatenate([x.reshape(batch * n, d), ctx.reshape(batch * n_ctx, d)], axis=0)
    for l in range(depth):
        with_ctx = l < depth - 1
        px = _in_projection(lay, l, xt, row3(g_mix), mod4, w_in_p)

        q_a, k_a, v_a = _mla_projection(lay, l, px, rope_mla, seg256, mla_w)
        mix_a = _mla_attention(lay, q_a, k_a, v_a, with_ctx)

        mix_b = _pool_mixer(lay, l, px, pool_w_b, row3(pool_scale), n, 0)
        if with_ctx:
            mix_b = _pool_mixer(lay, l, px, pool_w_b, row3(pool_scale), n_ctx, lay.n_lat // n_ctx, prev=mix_b)

        q_c, k_c, v_c = _swa_projection(lay, l, px, rope_swa, seg512, swa_qn, swa_kn)
        mix_c = _swa_attention(lay, l, q_c, k_c, v_c, swa_sink, with_ctx)

        q_d, k_d, v_d = _na_projection(lay, l, px, seg512, na_qn, na_kn)
        mix_d = _na_attention(lay, l, q_d, k_d, v_d, na_bias, with_ctx)

        xt = _out_projection(lay, l, xt, (mix_a, mix_b, mix_c, mix_d), w_out_b, mod4, with_ctx)
        xt = _conv_ffn(lay, l, xt, row3(g_ffn), mod4, w_up_b, ffn_conv_w, row3(ffn_conv_b), w_down_b, with_ctx)
    return xt[:lay.n_lat].reshape(batch, n, d)
```
